```python
import math
import jax, jax.numpy as jnp
from jax import lax
import numpy as np

D_MODEL = 2048
BATCH = 4
SEQ = 4096
DEPTH = 1

CHUNK = 64
MIX_WIDTH = D_MODEL
ATTN_WIDTH = MIX_WIDTH // 2
GMLP_WIDTH = MIX_WIDTH - ATTN_WIDTH
DIFF_HEAD_DIM = 64
DIFF_HEADS = ATTN_WIDTH // (2 * DIFF_HEAD_DIM)
Q_BLOCK = 128
GMLP_BLOCK = 128
GMLP_GROUPS = 8
GMLP_GROUP_DIM = GMLP_WIDTH // GMLP_GROUPS
IN_WIDTH = 3 * ATTN_WIDTH + 2 * GMLP_WIDTH
D_FF = 4 * D_MODEL
N_MOD = 6
DEEPNORM_ALPHA = (2.0 * DEPTH) ** 0.25
DEEPNORM_BETA = (8.0 * DEPTH) ** -0.25
LN_EPS = 1e-5

kernel_name = "hybrid_diffattn_gmlp_deepnorm_adaln"


def _ln(x):
    xf = x.astype(jnp.float32)
    mu = jnp.mean(xf, axis=-1, keepdims=True)
    var = jnp.mean(jnp.square(xf - mu), axis=-1, keepdims=True)
    return (xf - mu) * lax.rsqrt(var + LN_EPS)


def _rms(x):
    xf = x.astype(jnp.float32)
    return xf * lax.rsqrt(jnp.mean(jnp.square(xf), axis=-1, keepdims=True) + LN_EPS)


def diff_attention(q, k, v, lam, subln_g, lambda_init):
    b, s, h, _, dh = q.shape
    nblk = s // Q_BLOCK
    scale = dh ** -0.5
    kf = k.astype(jnp.float32)
    vf = v.astype(jnp.float32)
    k_chunk = jnp.arange(s) // CHUNK
    qb = q.astype(jnp.float32).reshape(b, nblk, Q_BLOCK, h, 2, dh).transpose(1, 0, 2, 3, 4, 5)
    neg = jnp.finfo(jnp.float32).min

    def one_block(args):
        q_blk, i = args
        q_chunk = (i * Q_BLOCK + jnp.arange(Q_BLOCK)) // CHUNK
        mask = k_chunk[None, :] <= q_chunk[:, None]
        sc = jnp.einsum('bqhcd,bkhcd->bhcqk', q_blk, kf) * scale
        p = jax.nn.softmax(jnp.where(mask, sc, neg), axis=-1)
        w = p[:, :, 0] - lam * p[:, :, 1]
        return jnp.einsum('bhqk,bkhe->bqhe', w, vf)

    out = lax.map(one_block, (qb, jnp.arange(nblk)))
    out = out.transpose(1, 0, 2, 3, 4).reshape(b, s, h, 2 * dh)
    out = _rms(out) * subln_g * (1.0 - lambda_init)
    return out.reshape(b, s, h * 2 * dh)


def gmlp_spatial_gate(u, vg, ln_g, ln_b, ws, bs):
    b, s, g, dg = u.shape
    nb = s // GMLP_BLOCK
    vn = _ln(vg) * ln_g + ln_b
    vb = vn.reshape(b, nb, GMLP_BLOCK, g, dg)
    pos = jnp.arange(GMLP_BLOCK)
    mask = (pos[:, None] // CHUNK) >= (pos[None, :] // CHUNK)
    wm = jnp.where(mask[None], ws.astype(jnp.float32), 0.0)
    mixed = jnp.einsum('gts,bnsgd->bntgd', wm, vb) + bs.T.astype(jnp.float32)[None, None, :, :, None]
    return (u.astype(jnp.float32) * mixed.reshape(b, s, g, dg)).reshape(b, s, g * dg)


def setup_inputs(seed: int = 0) -> dict:
    key = jax.random.key(seed)
    ks = jax.random.split(key, 24)
    f32 = jnp.float32
    n = lambda k, shape, s: jax.random.normal(k, shape, f32) * s
    L = DEPTH
    return {
        "x": jax.random.normal(ks[0], (BATCH, SEQ, D_MODEL), f32),
        "c": jax.random.normal(ks[1], (BATCH, D_MODEL), f32),
        "w_ada": n(ks[2], (L, D_MODEL, N_MOD * D_MODEL), 0.1 * D_MODEL ** -0.5),
        "b_ada": n(ks[3], (L, N_MOD * D_MODEL), 0.01),
        "w_in": n(ks[4], (L, D_MODEL, IN_WIDTH), D_MODEL ** -0.5),
        "lambda_q1": n(ks[5], (L, DIFF_HEAD_DIM), 0.1),
        "lambda_k1": n(ks[6], (L, DIFF_HEAD_DIM), 0.1),
        "lambda_q2": n(ks[7], (L, DIFF_HEAD_DIM), 0.1),
        "lambda_k2": n(ks[8], (L, DIFF_HEAD_DIM), 0.1),
        "subln_g": 1.0 + n(ks[9], (L, 2 * DIFF_HEAD_DIM), 0.02),
        "gmlp_ln_g": 1.0 + n(ks[10], (L, GMLP_GROUPS, GMLP_GROUP_DIM), 0.02),
        "gmlp_ln_b": n(ks[11], (L, GMLP_GROUPS, GMLP_GROUP_DIM), 0.02),
        "gmlp_ws": n(ks[12], (L, GMLP_GROUPS, GMLP_BLOCK, GMLP_BLOCK), GMLP_BLOCK ** -0.5),
        "gmlp_bs": 1.0 + n(ks[13], (L, GMLP_GROUPS, GMLP_BLOCK), 0.02),
        "w_out": n(ks[14], (L, MIX_WIDTH, D_MODEL), DEEPNORM_BETA * MIX_WIDTH ** -0.5),
        "ln1_g": 1.0 + n(ks[15], (L, D_MODEL), 0.02),
        "ln1_b": n(ks[16], (L, D_MODEL), 0.02),
        "w_ff1": n(ks[17], (L, D_MODEL, D_FF), D_MODEL ** -0.5),
        "w_ff2": n(ks[18], (L, D_FF, D_MODEL), DEEPNORM_BETA * D_FF ** -0.5),
        "ln2_g": 1.0 + n(ks[19], (L, D_MODEL), 0.02),
        "ln2_b": n(ks[20], (L, D_MODEL), 0.02),
    }


def reference(x, c, w_ada, b_ada, w_in, lambda_q1, lambda_k1, lambda_q2, lambda_k2, subln_g,
              gmlp_ln_g, gmlp_ln_b, gmlp_ws, gmlp_bs, w_out, ln1_g, ln1_b, w_ff1, w_ff2, ln2_g, ln2_b):
    b, s, d = x.shape
    out_dtype = x.dtype
    h_stream = x.astype(jnp.float32)
    c_act = jax.nn.silu(c.astype(jnp.float32))
    for l in range(DEPTH):
        mod = (c_act @ w_ada[l] + b_ada[l]).reshape(b, N_MOD, d)
        sh1, sc1, g1, sh2, sc2, g2 = [mod[:, i][:, None, :] for i in range(N_MOD)]

        a_in = _ln(h_stream) * (1.0 + sc1) + sh1
        proj = a_in @ w_in[l]
        q, k, v, u, vg = jnp.split(proj, [ATTN_WIDTH, 2 * ATTN_WIDTH, 3 * ATTN_WIDTH,
                                          3 * ATTN_WIDTH + GMLP_WIDTH], axis=-1)
        q = q.reshape(b, s, DIFF_HEADS, 2, DIFF_HEAD_DIM)
        k = k.reshape(b, s, DIFF_HEADS, 2, DIFF_HEAD_DIM)
        v = v.reshape(b, s, DIFF_HEADS, 2 * DIFF_HEAD_DIM)
        lambda_init = 0.8 - 0.6 * math.exp(-0.3 * l)
        lam = (jnp.exp(jnp.sum(lambda_q1[l] * lambda_k1[l]))
               - jnp.exp(jnp.sum(lambda_q2[l] * lambda_k2[l])) + lambda_init)
        attn_out = diff_attention(q, k, v, lam, subln_g[l], lambda_init)

        u = jax.nn.gelu(u).reshape(b, s, GMLP_GROUPS, GMLP_GROUP_DIM)
        vg = jax.nn.gelu(vg).reshape(b, s, GMLP_GROUPS, GMLP_GROUP_DIM)
        gmlp_out = gmlp_spatial_gate(u, vg, gmlp_ln_g[l], gmlp_ln_b[l], gmlp_ws[l], gmlp_bs[l])

        mix = jnp.concatenate([attn_out, gmlp_out], axis=-1) @ w_out[l]
        h_stream = _ln(DEEPNORM_ALPHA * h_stream + (1.0 + g1) * mix) * ln1_g[l] + ln1_b[l]

        m_in = _ln(h_stream) * (1.0 + sc2) + sh2
        ff = jnp.square(jax.nn.relu(m_in @ w_ff1[l])) @ w_ff2[l]
        h_stream = _ln(DEEPNORM_ALPHA * h_stream + (1.0 + g2) * ff) * ln2_g[l] + ln2_b[l]
    return h_stream.astype(out_dtype)
```

```python
import functools
import math

import jax
import jax.numpy as jnp
from jax import lax
from jax.experimental import pallas as pl
from jax.experimental.pallas import tpu as pltpu

CHUNK = 64
DIFF_HEAD_DIM = 64
V_HEAD_DIM = 2 * DIFF_HEAD_DIM
GMLP_BLOCK = 128
GMLP_GROUP_DIM = 128
N_MOD = 6
DEPTH = 1
DEEPNORM_ALPHA = (2.0 * DEPTH) ** 0.25
LN_EPS = 1e-5

F32 = jnp.float32
BF16 = jnp.bfloat16

VMEM_LIMIT_BYTES = 56 * 1024 * 1024
LN_ROWS = 64


def _ln_rows(x):
    mu = jnp.mean(x, axis=-1, keepdims=True)
    xc = x - mu
    var = jnp.mean(xc * xc, axis=-1, keepdims=True)
    return xc * lax.rsqrt(var + LN_EPS)


def _mod_kernel(c_ref, w_ref, b_ref, o_ref):
    c = c_ref[...]
    c_act = c * jax.nn.sigmoid(c)
    o_ref[...] = jnp.dot(c_act.astype(BF16), w_ref[...].astype(BF16),
                         preferred_element_type=F32) + b_ref[...]


def _mod(c_pad, w_ada, b_ada, tn=1024):
    rows, d = c_pad.shape
    n = w_ada.shape[1]
    return pl.pallas_call(
        _mod_kernel,
        grid=(n // tn,),
        in_specs=[pl.BlockSpec((rows, d), lambda j: (0, 0)),
                  pl.BlockSpec((d, tn), lambda j: (0, j)),
                  pl.BlockSpec((1, tn), lambda j: (0, j))],
        out_specs=pl.BlockSpec((rows, tn), lambda j: (0, j)),
        out_shape=jax.ShapeDtypeStruct((rows, n), F32),
        compiler_params=pltpu.CompilerParams(
            dimension_semantics=("arbitrary",), vmem_limit_bytes=VMEM_LIMIT_BYTES),
        name="mod",
    )(c_pad, w_ada, b_ada)


def _inproj_kernel(x_ref, sc_ref, sh_ref, w_ref, o_ref, a_scr):
    tm = x_ref.shape[0]

    @pl.when(pl.program_id(1) == 0)
    def _():
        scale = 1.0 + sc_ref[0]
        shift = sh_ref[0]

        def body(r, carry):
            rows = pl.ds(pl.multiple_of(r * LN_ROWS, LN_ROWS), LN_ROWS)
            a_scr[rows, :] = (_ln_rows(x_ref[rows, :]) * scale + shift).astype(BF16)
            return carry

        lax.fori_loop(0, tm // LN_ROWS, body, 0)

    o_ref[...] = jnp.dot(a_scr[...], w_ref[...], preferred_element_type=F32).astype(o_ref.dtype)


def _inproj(x2, sc, sh, w, seq, tm=1024, tn=1024):
    m, d = x2.shape
    n = w.shape[1]
    per_b = seq // tm
    return pl.pallas_call(
        _inproj_kernel,
        grid=(m // tm, n // tn),
        in_specs=[pl.BlockSpec((tm, d), lambda i, j: (i, 0)),
                  pl.BlockSpec((1, 1, d), lambda i, j: (i // per_b, 0, 0)),
                  pl.BlockSpec((1, 1, d), lambda i, j: (i // per_b, 0, 0)),
                  pl.BlockSpec((d, tn), lambda i, j: (0, j))],
        out_specs=pl.BlockSpec((tm, tn), lambda i, j: (i, j)),
        out_shape=jax.ShapeDtypeStruct((m, n), BF16),
        scratch_shapes=[pltpu.VMEM((tm, d), BF16)],
        compiler_params=pltpu.CompilerParams(
            dimension_semantics=("arbitrary", "arbitrary"), vmem_limit_bytes=VMEM_LIMIT_BYTES),
        name="inproj",
    )(x2, sc, sh, w)


def _attn_kernel(q_ref, k_ref, v_ref, lq1_ref, lk1_ref, lq2_ref, lk2_ref, g_ref, o_ref, *, tq, lambda_init):
    i = pl.program_id(2)
    q = q_ref[...]
    lane = lax.broadcasted_iota(jnp.int32, (1, V_HEAD_DIM), 1)
    qs = q * jnp.asarray(DIFF_HEAD_DIM ** -0.5, q.dtype)
    zero = jnp.zeros_like(qs)
    q_maps = (jnp.where(lane < DIFF_HEAD_DIM, qs, zero), jnp.where(lane >= DIFF_HEAD_DIM, qs, zero))
    nt_dims = (((1,), (1,)), ((), ()))

    def step(j, carry, masked):
        rows = pl.ds(pl.multiple_of(j * tq, tq), tq)
        k = k_ref[rows, :]
        v = v_ref[rows, :]
        if masked:
            rq = lax.broadcasted_iota(jnp.int32, (tq, tq), 0) // CHUNK
            ck = lax.broadcasted_iota(jnp.int32, (tq, tq), 1) // CHUNK
            visible = ck <= rq
        out = []
        for mp in range(2):
            m, l, acc = carry[mp]
            s = lax.dot_general(q_maps[mp], k, nt_dims, preferred_element_type=F32)
            if masked:
                s = jnp.where(visible, s, jnp.finfo(F32).min)
            m_new = jnp.maximum(m, jnp.max(s, axis=-1, keepdims=True))
            alpha = jnp.exp(m - m_new)
            p = jnp.exp(s - m_new)
            l_new = alpha * l + jnp.sum(p, axis=-1, keepdims=True)
            acc_new = alpha * acc + jnp.dot(p.astype(BF16), v, preferred_element_type=F32)
            out.append((m_new, l_new, acc_new))
        return tuple(out)

    init_one = (jnp.full((tq, 1), -jnp.inf, F32), jnp.zeros((tq, 1), F32),
                jnp.zeros((tq, V_HEAD_DIM), F32))
    carry = lax.fori_loop(0, i, functools.partial(step, masked=False), (init_one, init_one))
    (_, l1, a1), (_, l2, a2) = step(i, carry, masked=True)

    lam = (jnp.exp(jnp.sum(lq1_ref[...] * lk1_ref[...], axis=-1, keepdims=True))
           - jnp.exp(jnp.sum(lq2_ref[...] * lk2_ref[...], axis=-1, keepdims=True)) + lambda_init)
    o = a1 / l1 - lam * (a2 / l2)
    o = o * lax.rsqrt(jnp.mean(o * o, axis=-1, keepdims=True) + LN_EPS)
    o_ref[...] = (o * g_ref[...] * (1.0 - lambda_init)).astype(o_ref.dtype)


def _attention(proj, lq1, lk1, lq2, lk2, subln_g, lambda_init, batch, seq, heads, tq=256):
    m = proj.shape[0]
    nq = seq // tq
    vec = lambda width: pl.BlockSpec((1, width), lambda b, h, i: (0, 0))
    return pl.pallas_call(
        functools.partial(_attn_kernel, tq=tq, lambda_init=lambda_init),
        grid=(batch, heads, nq),
        in_specs=[pl.BlockSpec((tq, V_HEAD_DIM), lambda b, h, i: (b * nq + i, h)),
                  pl.BlockSpec((seq, V_HEAD_DIM), lambda b, h, i: (b, heads + h)),
                  pl.BlockSpec((seq, V_HEAD_DIM), lambda b, h, i: (b, 2 * heads + h)),
                  vec(DIFF_HEAD_DIM), vec(DIFF_HEAD_DIM), vec(DIFF_HEAD_DIM), vec(DIFF_HEAD_DIM),
                  vec(V_HEAD_DIM)],
        out_specs=pl.BlockSpec((tq, V_HEAD_DIM), lambda b, h, i: (b * nq + i, h)),
        out_shape=jax.ShapeDtypeStruct((m, heads * V_HEAD_DIM), BF16),
        compiler_params=pltpu.CompilerParams(
            dimension_semantics=("arbitrary", "arbitrary", "arbitrary"),
            vmem_limit_bytes=VMEM_LIMIT_BYTES),
        name="attn",
    )(proj, proj, proj, lq1, lk1, lq2, lk2, subln_g)


def _gmlp_kernel(u_ref, vg_ref, ws_ref, bst_ref, lg_ref, lb_ref, o_ref, *, groups):
    rows = u_ref.shape[0]
    pos_t = lax.broadcasted_iota(jnp.int32, (GMLP_BLOCK, GMLP_BLOCK), 0) // CHUNK
    pos_s = lax.broadcasted_iota(jnp.int32, (GMLP_BLOCK, GMLP_BLOCK), 1) // CHUNK
    causal = pos_t >= pos_s
    for g in range(groups):
        cols = slice(g * GMLP_GROUP_DIM, (g + 1) * GMLP_GROUP_DIM)
        wm = jnp.where(causal, ws_ref[g], 0.0).astype(BF16)
        bias = bst_ref[:, g:g + 1]
        for blk in range(rows // GMLP_BLOCK):
            r = slice(blk * GMLP_BLOCK, (blk + 1) * GMLP_BLOCK)
            vgg = jax.nn.gelu(vg_ref[r, cols].astype(F32))
            vn = _ln_rows(vgg) * lg_ref[g:g + 1, :] + lb_ref[g:g + 1, :]
            mixed = jnp.dot(wm, vn.astype(BF16), preferred_element_type=F32) + bias
            ug = jax.nn.gelu(u_ref[r, cols].astype(F32))
            o_ref[r, cols] = (ug * mixed).astype(o_ref.dtype)


def _gmlp(proj, ws, bs_t, ln_g, ln_b, attn_width, rows=512):
    m = proj.shape[0]
    groups, dg = ln_g.shape
    width = groups * dg
    u_blk = 3 * attn_width // width
    full = lambda shape: pl.BlockSpec(shape, lambda i: (0,) * len(shape))
    return pl.pallas_call(
        functools.partial(_gmlp_kernel, groups=groups),
        grid=(m // rows,),
        in_specs=[pl.BlockSpec((rows, width), lambda i: (i, u_blk)),
                  pl.BlockSpec((rows, width), lambda i: (i, u_blk + 1)),
                  full(ws.shape), full(bs_t.shape), full(ln_g.shape), full(ln_b.shape)],
        out_specs=pl.BlockSpec((rows, width), lambda i: (i, 0)),
        out_shape=jax.ShapeDtypeStruct((m, width), BF16),
        compiler_params=pltpu.CompilerParams(
            dimension_semantics=("arbitrary",), vmem_limit_bytes=VMEM_LIMIT_BYTES),
        name="gmlp",
    )(proj, proj, ws, bs_t, ln_g, ln_b)


def _outproj_kernel(a_ref, gm_ref, x_ref, g1_ref, w_ref, lg_ref, lb_ref, o_ref, mix_scr):
    tm = x_ref.shape[0]
    ka = a_ref.shape[1]
    mix_scr[...] = (jnp.dot(a_ref[...], w_ref[:ka, :], preferred_element_type=F32)
                    + jnp.dot(gm_ref[...], w_ref[ka:, :], preferred_element_type=F32))
    gate = 1.0 + g1_ref[0]
    lg = lg_ref[...]
    lb = lb_ref[...]

    def body(r, carry):
        rows = pl.ds(pl.multiple_of(r * LN_ROWS, LN_ROWS), LN_ROWS)
        y = DEEPNORM_ALPHA * x_ref[rows, :] + gate * mix_scr[rows, :]
        o_ref[rows, :] = _ln_rows(y) * lg + lb
        return carry

    lax.fori_loop(0, tm // LN_ROWS, body, 0)


def _outproj(attn, gm, x2, g1, w, lg, lb, seq, tm=512):
    m, d = x2.shape
    ka, kg = attn.shape[1], gm.shape[1]
    per_b = seq // tm
    return pl.pallas_call(
        _outproj_kernel,
        grid=(m // tm,),
        in_specs=[pl.BlockSpec((tm, ka), lambda i: (i, 0)),
                  pl.BlockSpec((tm, kg), lambda i: (i, 0)),
                  pl.BlockSpec((tm, d), lambda i: (i, 0)),
                  pl.BlockSpec((1, 1, d), lambda i: (i // per_b, 0, 0)),
                  pl.BlockSpec((ka + kg, d), lambda i: (0, 0)),
                  pl.BlockSpec((1, d), lambda i: (0, 0)),
                  pl.BlockSpec((1, d), lambda i: (0, 0))],
        out_specs=pl.BlockSpec((tm, d), lambda i: (i, 0)),
        out_shape=jax.ShapeDtypeStruct((m, d), F32),
        scratch_shapes=[pltpu.VMEM((tm, d), F32)],
        compiler_params=pltpu.CompilerParams(
            dimension_semantics=("arbitrary",), vmem_limit_bytes=VMEM_LIMIT_BYTES),
        name="outproj",
    )(attn, gm, x2, g1, w, lg, lb)


def _ffn_kernel(h_ref, sc_ref, sh_ref, g2_ref, w1_ref, w2_ref, lg_ref, lb_ref, o_ref, min_scr, acc_scr):
    tm = h_ref.shape[0]
    f = pl.program_id(1)
    nf = pl.num_programs(1)

    @pl.when(f == 0)
    def _():
        scale = 1.0 + sc_ref[0]
        shift = sh_ref[0]

        def body(r, carry):
            rows = pl.ds(pl.multiple_of(r * LN_ROWS, LN_ROWS), LN_ROWS)
            min_scr[rows, :] = (_ln_rows(h_ref[rows, :]) * scale + shift).astype(BF16)
            return carry

        lax.fori_loop(0, tm // LN_ROWS, body, 0)

    hid = jnp.dot(min_scr[...], w1_ref[...], preferred_element_type=F32)
    hid = jnp.square(jnp.maximum(hid, 0.0)).astype(BF16)
    contrib = jnp.dot(hid, w2_ref[...], preferred_element_type=F32)

    @pl.when(f == 0)
    def _():
        acc_scr[...] = contrib

    @pl.when(f > 0)
    def _():
        acc_scr[...] += contrib

    @pl.when(f == nf - 1)
    def _():
        gate = 1.0 + g2_ref[0]
        lg = lg_ref[...]
        lb = lb_ref[...]

        def body(r, carry):
            rows = pl.ds(pl.multiple_of(r * LN_ROWS, LN_ROWS), LN_ROWS)
            y = DEEPNORM_ALPHA * h_ref[rows, :] + gate * acc_scr[rows, :]
            o_ref[rows, :] = _ln_rows(y) * lg + lb
            return carry

        lax.fori_loop(0, tm // LN_ROWS, body, 0)


def _ffn(h1, sc, sh, g2, w1, w2, lg, lb, seq, tm=512, tf=1024):
    m, d = h1.shape
    dff = w1.shape[1]
    per_b = seq // tm
    modspec = pl.BlockSpec((1, 1, d), lambda i, f: (i // per_b, 0, 0))
    return pl.pallas_call(
        _ffn_kernel,
        grid=(m // tm, dff // tf),
        in_specs=[pl.BlockSpec((tm, d), lambda i, f: (i, 0)),
                  modspec, modspec, modspec,
                  pl.BlockSpec((d, tf), lambda i, f: (0, f)),
                  pl.BlockSpec((tf, d), lambda i, f: (f, 0)),
                  pl.BlockSpec((1, d), lambda i, f: (0, 0)),
                  pl.BlockSpec((1, d), lambda i, f: (0, 0))],
        out_specs=pl.BlockSpec((tm, d), lambda i, f: (i, 0)),
        out_shape=jax.ShapeDtypeStruct((m, d), F32),
        scratch_shapes=[pltpu.VMEM((tm, d), BF16), pltpu.VMEM((tm, d), F32)],
        compiler_params=pltpu.CompilerParams(
            dimension_semantics=("arbitrary", "arbitrary"), vmem_limit_bytes=VMEM_LIMIT_BYTES),
        name="ffn",
    )(h1, sc, sh, g2, w1, w2, lg, lb)


def kernel(x, c, w_ada, b_ada, w_in, lambda_q1, lambda_k1, lambda_q2, lambda_k2, subln_g,
           gmlp_ln_g, gmlp_ln_b, gmlp_ws, gmlp_bs, w_out, ln1_g, ln1_b, w_ff1, w_ff2, ln2_g, ln2_b):
    b, s, d = x.shape
    assert w_ada.shape[0] == DEPTH
    gmlp_width = gmlp_ln_g.shape[1] * gmlp_ln_g.shape[2]
    attn_width = (w_in.shape[-1] - 2 * gmlp_width) // 3
    heads = attn_width // V_HEAD_DIM
    x2 = x.astype(F32).reshape(b * s, d)
    c_pad = jnp.pad(c.astype(F32), ((0, (-b) % 8), (0, 0)))
    row = lambda a: a.reshape(1, -1)

    h = x2
    for l in range(DEPTH):
        mod = _mod(c_pad, w_ada[l], row(b_ada[l]))[:b].reshape(b, N_MOD, 1, d)
        sh1, sc1, g1, sh2, sc2, g2 = [mod[:, i] for i in range(N_MOD)]

        proj = _inproj(h, sc1, sh1, w_in[l].astype(BF16), s)
        lambda_init = 0.8 - 0.6 * math.exp(-0.3 * l)
        attn = _attention(proj, row(lambda_q1[l]), row(lambda_k1[l]), row(lambda_q2[l]),
                          row(lambda_k2[l]), row(subln_g[l]), lambda_init, b, s, heads)
        gm = _gmlp(proj, gmlp_ws[l], gmlp_bs[l].T, gmlp_ln_g[l], gmlp_ln_b[l], attn_width)
        h1 = _outproj(attn, gm, h, g1, w_out[l].astype(BF16), row(ln1_g[l]), row(ln1_b[l]), s)
        h = _ffn(h1, sc2, sh2, g2, w_ff1[l].astype(BF16), w_ff2[l].astype(BF16),
                 row(ln2_g[l]), row(ln2_b[l]), s)
    return h.reshape(b, s, d).astype(x.dtype)
```

```python
import functools
import math

import jax
import jax.numpy as jnp
from jax import lax
from jax.experimental import pallas as pl
from jax.experimental.pallas import tpu as pltpu

CHUNK = 64
DIFF_HEAD_DIM = 64
V_HEAD_DIM = 2 * DIFF_HEAD_DIM
GMLP_BLOCK = 128
GMLP_GROUP_DIM = 128
N_MOD = 6
DEPTH = 1
DEEPNORM_ALPHA = (2.0 * DEPTH) ** 0.25
LN_EPS = 1e-5
LOG2E = math.log2(math.e)

F32 = jnp.float32
BF16 = jnp.bfloat16

VMEM_LIMIT_BYTES = 56 * 1024 * 1024
LN_ROWS = 64


def _ln_rows(x):
    mu = jnp.mean(x, axis=-1, keepdims=True)
    xc = x - mu
    var = jnp.mean(xc * xc, axis=-1, keepdims=True)
    return xc * lax.rsqrt(var + LN_EPS)


def _mod_kernel(c_ref, w_ref, b_ref, o_ref):
    c = c_ref[...]
    c_act = c * jax.nn.sigmoid(c)
    o_ref[...] = jnp.dot(c_act.astype(BF16), w_ref[...].astype(BF16),
                         preferred_element_type=F32) + b_ref[...]


def _mod(c_pad, w_ada, b_ada, tn=1024):
    rows, d = c_pad.shape
    n = w_ada.shape[1]
    return pl.pallas_call(
        _mod_kernel,
        grid=(n // tn,),
        in_specs=[pl.BlockSpec((rows, d), lambda j: (0, 0)),
                  pl.BlockSpec((d, tn), lambda j: (0, j)),
                  pl.BlockSpec((1, tn), lambda j: (0, j))],
        out_specs=pl.BlockSpec((rows, tn), lambda j: (0, j)),
        out_shape=jax.ShapeDtypeStruct((rows, n), F32),
        compiler_params=pltpu.CompilerParams(
            dimension_semantics=("arbitrary",), vmem_limit_bytes=VMEM_LIMIT_BYTES),
        name="mod",
    )(c_pad, w_ada, b_ada)


def _inproj_kernel(x_ref, sc_ref, sh_ref, w_ref, o_ref, a_scr):
    tm = x_ref.shape[0]

    @pl.when(pl.program_id(1) == 0)
    def _():
        scale = 1.0 + sc_ref[0]
        shift = sh_ref[0]

        def body(r, carry):
            rows = pl.ds(pl.multiple_of(r * LN_ROWS, LN_ROWS), LN_ROWS)
            a_scr[rows, :] = (_ln_rows(x_ref[rows, :]) * scale + shift).astype(BF16)
            return carry

        lax.fori_loop(0, tm // LN_ROWS, body, 0)

    o_ref[...] = jnp.dot(a_scr[...], w_ref[...], preferred_element_type=F32).astype(o_ref.dtype)


def _inproj(x2, sc, sh, w, seq, tm=1024, tn=1024):
    m, d = x2.shape
    n = w.shape[1]
    per_b = seq // tm
    return pl.pallas_call(
        _inproj_kernel,
        grid=(m // tm, n // tn),
        in_specs=[pl.BlockSpec((tm, d), lambda i, j: (i, 0)),
                  pl.BlockSpec((1, 1, d), lambda i, j: (i // per_b, 0, 0)),
                  pl.BlockSpec((1, 1, d), lambda i, j: (i // per_b, 0, 0)),
                  pl.BlockSpec((d, tn), lambda i, j: (0, j))],
        out_specs=pl.BlockSpec((tm, tn), lambda i, j: (i, j)),
        out_shape=jax.ShapeDtypeStruct((m, n), BF16),
        scratch_shapes=[pltpu.VMEM((tm, d), BF16)],
        compiler_params=pltpu.CompilerParams(
            dimension_semantics=("arbitrary", "arbitrary"), vmem_limit_bytes=VMEM_LIMIT_BYTES),
        name="inproj",
    )(x2, sc, sh, w)


def _attn_kernel(q_ref, k_ref, v_ref, lq1_ref, lk1_ref, lq2_ref, lk2_ref, g_ref, o_ref,
                 qst_scr, vext_scr, m_scr, acc_scr, *, tq, lambda_init):
    i = pl.program_id(2)
    seq = k_ref.shape[0]
    vd = V_HEAD_DIM
    nt_dims = (((1,), (1,)), ((), ()))

    @pl.when(i == 0)
    def _():
        vext_scr[:, :vd] = v_ref[...]
        lane = lax.broadcasted_iota(jnp.int32, (seq, vd), 1)
        vext_scr[:, vd:] = jnp.where(lane == 0, 1.0, 0.0).astype(BF16)

    q = q_ref[...].astype(F32) * (DIFF_HEAD_DIM ** -0.5 * LOG2E)
    lane = lax.broadcasted_iota(jnp.int32, (tq, vd), 1)
    qst_scr[:tq, :] = jnp.where(lane < DIFF_HEAD_DIM, q, 0.0).astype(BF16)
    qst_scr[tq:, :] = jnp.where(lane >= DIFF_HEAD_DIM, q, 0.0).astype(BF16)
    m_scr[...] = jnp.full(m_scr.shape, -jnp.inf, F32)
    acc_scr[...] = jnp.zeros(acc_scr.shape, F32)

    def kv_rows(j):
        return pl.ds(pl.multiple_of(j * tq, tq), tq)

    def scores(j):
        return lax.dot_general(qst_scr[...], k_ref[kv_rows(j), :], nt_dims, preferred_element_type=F32)

    def softmax_pv(s, j, masked):
        rows = kv_rows(j)
        if masked:
            rq = lax.broadcasted_iota(jnp.int32, (tq, tq), 0) // CHUNK
            ck = lax.broadcasted_iota(jnp.int32, (tq, tq), 1) // CHUNK
            visible = ck <= rq
            s = jnp.where(jnp.concatenate([visible, visible], axis=0), s, jnp.finfo(F32).min)
        tiles = [s[:, t * vd:(t + 1) * vd] for t in range(tq // vd)]
        mx = functools.reduce(jnp.maximum, tiles)
        m_old = m_scr[...]
        m_new = jnp.maximum(m_old, jnp.max(mx, axis=-1, keepdims=True))
        alpha = jnp.exp2(m_old - m_new)
        m_scr[...] = m_new
        p = jnp.concatenate([jnp.exp2(t - m_new).astype(BF16) for t in tiles], axis=-1)
        pv = jnp.dot(p, vext_scr[rows, :], preferred_element_type=F32)
        acc_scr[:, :vd] = alpha * acc_scr[:, :vd] + pv[:, :vd]
        acc_scr[:, vd:] = alpha * acc_scr[:, vd:] + pv[:, vd:]

    def body(j, s):
        s_next = scores(j + 1)
        softmax_pv(s, j, masked=False)
        return s_next

    s_diag = lax.fori_loop(0, i, body, scores(0))
    softmax_pv(s_diag, i, masked=True)

    lam = (jnp.exp(jnp.sum(lq1_ref[...] * lk1_ref[...], axis=-1, keepdims=True))
           - jnp.exp(jnp.sum(lq2_ref[...] * lk2_ref[...], axis=-1, keepdims=True)) + lambda_init)
    o = (acc_scr[:tq, :vd] / acc_scr[:tq, vd:vd + 1]
         - lam * (acc_scr[tq:, :vd] / acc_scr[tq:, vd:vd + 1]))
    o = o * lax.rsqrt(jnp.mean(o * o, axis=-1, keepdims=True) + LN_EPS)
    o_ref[...] = (o * g_ref[...] * (1.0 - lambda_init)).astype(o_ref.dtype)


def _attention(proj, lq1, lk1, lq2, lk2, subln_g, lambda_init, batch, seq, heads, tq=512):
    m = proj.shape[0]
    nq = seq // tq
    vd = V_HEAD_DIM
    vec = lambda width: pl.BlockSpec((1, width), lambda b, h, i: (0, 0))
    return pl.pallas_call(
        functools.partial(_attn_kernel, tq=tq, lambda_init=lambda_init),
        grid=(batch, heads, nq),
        in_specs=[pl.BlockSpec((tq, vd), lambda b, h, i: (b * nq + i, h)),
                  pl.BlockSpec((seq, vd), lambda b, h, i: (b, heads + h)),
                  pl.BlockSpec((seq, vd), lambda b, h, i: (b, 2 * heads + h)),
                  vec(DIFF_HEAD_DIM), vec(DIFF_HEAD_DIM), vec(DIFF_HEAD_DIM), vec(DIFF_HEAD_DIM),
                  vec(vd)],
        out_specs=pl.BlockSpec((tq, vd), lambda b, h, i: (b * nq + i, h)),
        out_shape=jax.ShapeDtypeStruct((m, heads * vd), BF16),
        scratch_shapes=[pltpu.VMEM((2 * tq, vd), BF16),
                        pltpu.VMEM((seq, 2 * vd), BF16),
                        pltpu.VMEM((2 * tq, vd), F32),
                        pltpu.VMEM((2 * tq, 2 * vd), F32)],
        compiler_params=pltpu.CompilerParams(
            dimension_semantics=("arbitrary", "arbitrary", "arbitrary"),
            vmem_limit_bytes=VMEM_LIMIT_BYTES),
        name="attn",
    )(proj, proj, proj, lq1, lk1, lq2, lk2, subln_g)


def _gmlp_kernel(u_ref, vg_ref, ws_ref, bst_ref, lg_ref, lb_ref, o_ref, *, groups):
    rows = u_ref.shape[0]
    pos_t = lax.broadcasted_iota(jnp.int32, (GMLP_BLOCK, GMLP_BLOCK), 0) // CHUNK
    pos_s = lax.broadcasted_iota(jnp.int32, (GMLP_BLOCK, GMLP_BLOCK), 1) // CHUNK
    causal = pos_t >= pos_s
    for g in range(groups):
        cols = slice(g * GMLP_GROUP_DIM, (g + 1) * GMLP_GROUP_DIM)
        wm = jnp.where(causal, ws_ref[g], 0.0).astype(BF16)
        bias = bst_ref[:, g:g + 1]
        for blk in range(rows // GMLP_BLOCK):
            r = slice(blk * GMLP_BLOCK, (blk + 1) * GMLP_BLOCK)
            vgg = jax.nn.gelu(vg_ref[r, cols].astype(F32))
            vn = _ln_rows(vgg) * lg_ref[g:g + 1, :] + lb_ref[g:g + 1, :]
            mixed = jnp.dot(wm, vn.astype(BF16), preferred_element_type=F32) + bias
            ug = jax.nn.gelu(u_ref[r, cols].astype(F32))
            o_ref[r, cols] = (ug * mixed).astype(o_ref.dtype)


def _gmlp(proj, ws, bs_t, ln_g, ln_b, attn_width, rows=512):
    m = proj.shape[0]
    groups, dg = ln_g.shape
    width = groups * dg
    u_blk = 3 * attn_width // width
    full = lambda shape: pl.BlockSpec(shape, lambda i: (0,) * len(shape))
    return pl.pallas_call(
        functools.partial(_gmlp_kernel, groups=groups),
        grid=(m // rows,),
        in_specs=[pl.BlockSpec((rows, width), lambda i: (i, u_blk)),
                  pl.BlockSpec((rows, width), lambda i: (i, u_blk + 1)),
                  full(ws.shape), full(bs_t.shape), full(ln_g.shape), full(ln_b.shape)],
        out_specs=pl.BlockSpec((rows, width), lambda i: (i, 0)),
        out_shape=jax.ShapeDtypeStruct((m, width), BF16),
        compiler_params=pltpu.CompilerParams(
            dimension_semantics=("arbitrary",), vmem_limit_bytes=VMEM_LIMIT_BYTES),
        name="gmlp",
    )(proj, proj, ws, bs_t, ln_g, ln_b)


def _outproj_kernel(a_ref, gm_ref, x_ref, g1_ref, w_ref, lg_ref, lb_ref, o_ref, mix_scr):
    tm = x_ref.shape[0]
    ka = a_ref.shape[1]
    mix_scr[...] = (jnp.dot(a_ref[...], w_ref[:ka, :], preferred_element_type=F32)
                    + jnp.dot(gm_ref[...], w_ref[ka:, :], preferred_element_type=F32))
    gate = 1.0 + g1_ref[0]
    lg = lg_ref[...]
    lb = lb_ref[...]

    def body(r, carry):
        rows = pl.ds(pl.multiple_of(r * LN_ROWS, LN_ROWS), LN_ROWS)
        y = DEEPNORM_ALPHA * x_ref[rows, :] + gate * mix_scr[rows, :]
        o_ref[rows, :] = _ln_rows(y) * lg + lb
        return carry

    lax.fori_loop(0, tm // LN_ROWS, body, 0)


def _outproj(attn, gm, x2, g1, w, lg, lb, seq, tm=512):
    m, d = x2.shape
    ka, kg = attn.shape[1], gm.shape[1]
    per_b = seq // tm
    return pl.pallas_call(
        _outproj_kernel,
        grid=(m // tm,),
        in_specs=[pl.BlockSpec((tm, ka), lambda i: (i, 0)),
                  pl.BlockSpec((tm, kg), lambda i: (i, 0)),
                  pl.BlockSpec((tm, d), lambda i: (i, 0)),
                  pl.BlockSpec((1, 1, d), lambda i: (i // per_b, 0, 0)),
                  pl.BlockSpec((ka + kg, d), lambda i: (0, 0)),
                  pl.BlockSpec((1, d), lambda i: (0, 0)),
                  pl.BlockSpec((1, d), lambda i: (0, 0))],
        out_specs=pl.BlockSpec((tm, d), lambda i: (i, 0)),
        out_shape=jax.ShapeDtypeStruct((m, d), F32),
        scratch_shapes=[pltpu.VMEM((tm, d), F32)],
        compiler_params=pltpu.CompilerParams(
            dimension_semantics=("arbitrary",), vmem_limit_bytes=VMEM_LIMIT_BYTES),
        name="outproj",
    )(attn, gm, x2, g1, w, lg, lb)


def _ffn_kernel(h_ref, sc_ref, sh_ref, g2_ref, w1_ref, w2_ref, lg_ref, lb_ref, o_ref, min_scr, acc_scr):
    tm = h_ref.shape[0]
    f = pl.program_id(1)
    nf = pl.num_programs(1)

    @pl.when(f == 0)
    def _():
        scale = 1.0 + sc_ref[0]
        shift = sh_ref[0]

        def body(r, carry):
            rows = pl.ds(pl.multiple_of(r * LN_ROWS, LN_ROWS), LN_ROWS)
            min_scr[rows, :] = (_ln_rows(h_ref[rows, :]) * scale + shift).astype(BF16)
            return carry

        lax.fori_loop(0, tm // LN_ROWS, body, 0)

    hid = jnp.dot(min_scr[...], w1_ref[...], preferred_element_type=F32)
    hid = jnp.square(jnp.maximum(hid, 0.0)).astype(BF16)
    contrib = jnp.dot(hid, w2_ref[...], preferred_element_type=F32)

    @pl.when(f == 0)
    def _():
        acc_scr[...] = contrib

    @pl.when(f > 0)
    def _():
        acc_scr[...] += contrib

    @pl.when(f == nf - 1)
    def _():
        gate = 1.0 + g2_ref[0]
        lg = lg_ref[...]
        lb = lb_ref[...]

        def body(r, carry):
            rows = pl.ds(pl.multiple_of(r * LN_ROWS, LN_ROWS), LN_ROWS)
            y = DEEPNORM_ALPHA * h_ref[rows, :] + gate * acc_scr[rows, :]
            o_ref[rows, :] = _ln_rows(y) * lg + lb
            return carry

        lax.fori_loop(0, tm // LN_ROWS, body, 0)


def _ffn(h1, sc, sh, g2, w1, w2, lg, lb, seq, tm=512, tf=1024):
    m, d = h1.shape
    dff = w1.shape[1]
    per_b = seq // tm
    modspec = pl.BlockSpec((1, 1, d), lambda i, f: (i // per_b, 0, 0))
    return pl.pallas_call(
        _ffn_kernel,
        grid=(m // tm, dff // tf),
        in_specs=[pl.BlockSpec((tm, d), lambda i, f: (i, 0)),
                  modspec, modspec, modspec,
                  pl.BlockSpec((d, tf), lambda i, f: (0, f)),
                  pl.BlockSpec((tf, d), lambda i, f: (f, 0)),
                  pl.BlockSpec((1, d), lambda i, f: (0, 0)),
                  pl.BlockSpec((1, d), lambda i, f: (0, 0))],
        out_specs=pl.BlockSpec((tm, d), lambda i, f: (i, 0)),
        out_shape=jax.ShapeDtypeStruct((m, d), F32),
        scratch_shapes=[pltpu.VMEM((tm, d), BF16), pltpu.VMEM((tm, d), F32)],
        compiler_params=pltpu.CompilerParams(
            dimension_semantics=("arbitrary", "arbitrary"), vmem_limit_bytes=VMEM_LIMIT_BYTES),
        name="ffn",
    )(h1, sc, sh, g2, w1, w2, lg, lb)


def kernel(x, c, w_ada, b_ada, w_in, lambda_q1, lambda_k1, lambda_q2, lambda_k2, subln_g,
           gmlp_ln_g, gmlp_ln_b, gmlp_ws, gmlp_bs, w_out, ln1_g, ln1_b, w_ff1, w_ff2, ln2_g, ln2_b):
    b, s, d = x.shape
    assert w_ada.shape[0] == DEPTH
    gmlp_width = gmlp_ln_g.shape[1] * gmlp_ln_g.shape[2]
    attn_width = (w_in.shape[-1] - 2 * gmlp_width) // 3
    heads = attn_width // V_HEAD_DIM
    x2 = x.astype(F32).reshape(b * s, d)
    c_pad = jnp.pad(c.astype(F32), ((0, (-b) % 8), (0, 0)))
    row = lambda a: a.reshape(1, -1)

    h = x2
    for l in range(DEPTH):
        mod = _mod(c_pad, w_ada[l], row(b_ada[l]))[:b].reshape(b, N_MOD, 1, d)
        sh1, sc1, g1, sh2, sc2, g2 = [mod[:, i] for i in range(N_MOD)]

        proj = _inproj(h, sc1, sh1, w_in[l].astype(BF16), s)
        lambda_init = 0.8 - 0.6 * math.exp(-0.3 * l)
        attn = _attention(proj, row(lambda_q1[l]), row(lambda_k1[l]), row(lambda_q2[l]),
                          row(lambda_k2[l]), row(subln_g[l]), lambda_init, b, s, heads)
        gm = _gmlp(proj, gmlp_ws[l], gmlp_bs[l].T, gmlp_ln_g[l], gmlp_ln_b[l], attn_width)
        h1 = _outproj(attn, gm, h, g1, w_out[l].astype(BF16), row(ln1_g[l]), row(ln1_b[l]), s)
        h = _ffn(h1, sc2, sh2, g2, w_ff1[l].astype(BF16), w_ff2[l].astype(BF16),
                 row(ln2_g[l]), row(ln2_b[l]), s)
    return h.reshape(b, s, d).astype(x.dtype)
```
